```python
import jax, jax.numpy as jnp
from jax import lax
import numpy as np

D_MODEL = 1024
BATCH = 8
SEQ = 2048
DEPTH = 4

EXPAND = 2
E_WIDTH = EXPAND * D_MODEL
CONV_W = 3
CHUNK = 128
SG_HEADS = 8
SG_HEAD_DIM = E_WIDTH // SG_HEADS
N_MIXERS = 2
N_A = (DEPTH + 1) // 2
N_B = DEPTH // 2
EPS = 1e-6

kernel_name = "hybrid_shortconv_gmlp_adaln_encoder"


def _rmsnorm(x, g):
    xf = x.astype(jnp.float32)
    y = xf * lax.rsqrt(jnp.mean(xf * xf, axis=-1, keepdims=True) + EPS)
    return (y * g.astype(jnp.float32)).astype(x.dtype)


def _layernorm(x, g, b):
    xf = x.astype(jnp.float32)
    mu = jnp.mean(xf, axis=-1, keepdims=True)
    var = jnp.mean(jnp.square(xf - mu), axis=-1, keepdims=True)
    y = (xf - mu) * lax.rsqrt(var + EPS)
    return (y * g.astype(jnp.float32) + b.astype(jnp.float32)).astype(x.dtype)


def _modulation(c, w_mod, b_mod):
    m = jax.nn.silu(c) @ w_mod + b_mod
    shift, scale, gate = jnp.split(m, 3, axis=-1)
    return shift[:, None, :], scale[:, None, :], gate[:, None, :]


def _short_conv_mixer(h, w_in, w_conv, w_out):
    proj = h @ w_in
    z, bg, cg, xh = jnp.split(proj, 4, axis=-1)
    y = cg * xh
    yp = jnp.pad(y, ((0, 0), (1, 1), (0, 0)))
    y = (yp[:, :-2] * w_conv[:, 0] + yp[:, 1:-1] * w_conv[:, 1]
         + yp[:, 2:] * w_conv[:, 2])
    y = bg * y * jax.nn.silu(z)
    return y @ w_out


def _spatial_gating_mixer(h, w_in, ln_g, ln_b, w_s, b_s, w_out):
    bsz, s, _ = h.shape
    proj = h @ w_in
    z = proj[..., :E_WIDTH]
    uv = jax.nn.gelu(proj[..., E_WIDTH:], approximate=False)
    u, v = jnp.split(uv, 2, axis=-1)
    v = _layernorm(v, ln_g, ln_b)
    vc = v.reshape(bsz, s // CHUNK, CHUNK, SG_HEADS, SG_HEAD_DIM)
    sv = jnp.einsum('gpq,bnqge->bnpge', w_s, vc) + b_s.T[None, None, :, :, None]
    sv = sv.reshape(bsz, s, E_WIDTH)
    y = u * sv * jax.nn.silu(z)
    return y @ w_out


def setup_inputs(seed: int = 0) -> dict:
    key = jax.random.key(seed)
    ks = jax.random.split(key, 16)
    f32 = jnp.float32
    D, E = D_MODEL, E_WIDTH
    x = jax.random.normal(ks[0], (BATCH, SEQ, D), f32)
    c = jax.random.normal(ks[1], (BATCH, D), f32)
    norm_g = 1.0 + 0.05 * jax.random.normal(ks[2], (DEPTH, D), f32)
    w_mod = 0.2 * D ** -0.5 * jax.random.normal(ks[3], (DEPTH, D, 3 * D), f32)
    b_mod = 0.02 * jax.random.normal(ks[4], (DEPTH, 3 * D), f32)
    a_w_in = D ** -0.5 * jax.random.normal(ks[5], (N_A, D, 4 * E), f32)
    a_w_conv = CONV_W ** -0.5 * jax.random.normal(ks[6], (N_A, E, CONV_W), f32)
    a_w_out = E ** -0.5 * jax.random.normal(ks[7], (N_A, E, D), f32)
    b_w_in = D ** -0.5 * jax.random.normal(ks[8], (N_B, D, 3 * E), f32)
    b_ln_g = 1.0 + 0.05 * jax.random.normal(ks[9], (N_B, E), f32)
    b_ln_b = 0.02 * jax.random.normal(ks[10], (N_B, E), f32)
    b_w_s = CHUNK ** -0.5 * jax.random.normal(ks[11], (N_B, SG_HEADS, CHUNK, CHUNK), f32)
    b_b_s = 1.0 + 0.1 * jax.random.normal(ks[12], (N_B, SG_HEADS, CHUNK), f32)
    b_w_out = E ** -0.5 * jax.random.normal(ks[13], (N_B, E, D), f32)
    final_g = 1.0 + 0.05 * jax.random.normal(ks[14], (D,), f32)
    return {"x": x, "c": c, "norm_g": norm_g, "w_mod": w_mod, "b_mod": b_mod,
            "a_w_in": a_w_in, "a_w_conv": a_w_conv, "a_w_out": a_w_out,
            "b_w_in": b_w_in, "b_ln_g": b_ln_g, "b_ln_b": b_ln_b,
            "b_w_s": b_w_s, "b_b_s": b_b_s, "b_w_out": b_w_out,
            "final_g": final_g}


def reference(x, c, norm_g, w_mod, b_mod, a_w_in, a_w_conv, a_w_out,
              b_w_in, b_ln_g, b_ln_b, b_w_s, b_b_s, b_w_out, final_g):
    for i in range(DEPTH):
        shift, scale, gate = _modulation(c, w_mod[i], b_mod[i])
        h = _rmsnorm(x, norm_g[i]) * (1.0 + scale) + shift
        j = i // N_MIXERS
        if i % N_MIXERS == 0:
            y = _short_conv_mixer(h, a_w_in[j], a_w_conv[j], a_w_out[j])
        else:
            y = _spatial_gating_mixer(h, b_w_in[j], b_ln_g[j], b_ln_b[j],
                                      b_w_s[j], b_b_s[j], b_w_out[j])
        x = x + gate * y
    return _rmsnorm(x, final_g)
```

```python
import functools

import jax
import jax.numpy as jnp
from jax import lax
from jax.experimental import pallas as pl
from jax.experimental.pallas import tpu as pltpu

EPS = 1e-6
CHUNK = 128
SG_HEADS = 8

ROW_TILE = 1024
COL_TILE = 256
HALO = 16
MOD_COL_TILE = 1024
VMEM_LIMIT_BYTES = 52 * 1024 * 1024

_BF16 = jnp.bfloat16
_F32 = jnp.float32


def _silu(x):
    return x * jax.nn.sigmoid(x)


def _gelu_exact(x):
    return 0.5 * x * (1.0 + lax.erf(x * 0.7071067811865476))


def _rmsnorm(x, g):
    return x * lax.rsqrt(jnp.mean(x * x, axis=-1, keepdims=True) + EPS) * g


def _norm_modulate(x, g, mod_ref):
    return _rmsnorm(x, g) * (1.0 + mod_ref[0, 1:2, :]) + mod_ref[0, 0:1, :]


def _modulation_kernel(c_ref, w_ref, b_ref, o_ref):
    s = _silu(c_ref[...]).astype(_BF16)
    w = w_ref[0].astype(_BF16)
    o_ref[0] = jnp.dot(s, w, preferred_element_type=_F32) + b_ref[0]


def _modulation(c, w_mod, b_mod):
    depth, d, n = w_mod.shape
    bsz = c.shape[0]
    return pl.pallas_call(
        _modulation_kernel,
        grid=(depth, n // MOD_COL_TILE),
        in_specs=[
            pl.BlockSpec((bsz, d), lambda l, k: (0, 0)),
            pl.BlockSpec((1, d, MOD_COL_TILE), lambda l, k: (l, 0, k)),
            pl.BlockSpec((1, 1, MOD_COL_TILE), lambda l, k: (l, 0, k)),
        ],
        out_specs=pl.BlockSpec((1, bsz, MOD_COL_TILE), lambda l, k: (l, 0, k)),
        out_shape=jax.ShapeDtypeStruct((depth, bsz, n), _F32),
        compiler_params=pltpu.CompilerParams(
            dimension_semantics=("arbitrary", "arbitrary")),
        name="modulation",
    )(c, w_mod, b_mod.reshape(depth, 1, n))


def _layer_a_kernel(x_ref, xp_ref, xn_ref, mod_ref, ng_ref, wz_ref, wb_ref, wc_ref,
                    wx_ref, wconv_ref, wout_ref, o_ref, h_ref):
    i = pl.program_id(1)
    j = pl.program_id(2)
    tm = x_ref.shape[1]

    @pl.when(j == 0)
    def _():
        g = ng_ref[...]
        h_ref[HALO:HALO + tm, :] = _norm_modulate(x_ref[0], g, mod_ref).astype(_BF16)
        hp = _norm_modulate(xp_ref[0, 0], g, mod_ref)
        h_ref[0:HALO, :] = jnp.where(i > 0, hp, 0.0).astype(_BF16)
        hn = _norm_modulate(xn_ref[0, 0], g, mod_ref)
        h_ref[HALO + tm:, :] = jnp.where(i < pl.num_programs(1) - 1, hn, 0.0).astype(_BF16)
        o_ref[...] = jnp.zeros_like(o_ref)

    h_ext = h_ref[...]
    h_mid = h_ref[HALO:HALO + tm, :]
    z = jnp.dot(h_mid, wz_ref[...], preferred_element_type=_F32)
    bg = jnp.dot(h_mid, wb_ref[...], preferred_element_type=_F32)
    cg = jnp.dot(h_ext, wc_ref[...], preferred_element_type=_F32)
    xh = jnp.dot(h_ext, wx_ref[...], preferred_element_type=_F32)
    yc = cg * xh
    n_ext = tm + 2 * HALO
    prev = pltpu.roll(yc, 1, 0)[HALO:HALO + tm]
    nxt = pltpu.roll(yc, n_ext - 1, 0)[HALO:HALO + tm]
    conv = (prev * wconv_ref[0:1, :] + yc[HALO:HALO + tm] * wconv_ref[1:2, :]
            + nxt * wconv_ref[2:3, :])
    y = bg * conv * _silu(z)
    o_ref[0] += jnp.dot(y.astype(_BF16), wout_ref[...], preferred_element_type=_F32)

    @pl.when(j == pl.num_programs(2) - 1)
    def _():
        o_ref[0] = x_ref[0] + mod_ref[0, 2:3, :] * o_ref[0]


def _layer_a(x, mod, norm_g, w_in, w_conv_t, w_out):
    bsz, s, d = x.shape
    e = w_out.shape[0]
    n_i, n_j = s // ROW_TILE, e // COL_TILE
    halo_per_tile = ROW_TILE // HALO
    x_halo = x.reshape(bsz, s // HALO, HALO, d)

    def w_in_spec(section):
        return pl.BlockSpec((d, COL_TILE), lambda b, i, j: (0, section * n_j + j))

    return pl.pallas_call(
        _layer_a_kernel,
        grid=(bsz, n_i, n_j),
        in_specs=[
            pl.BlockSpec((1, ROW_TILE, d), lambda b, i, j: (b, i, 0)),
            pl.BlockSpec((1, 1, HALO, d),
                         lambda b, i, j: (b, jnp.maximum(i * halo_per_tile - 1, 0), 0, 0)),
            pl.BlockSpec((1, 1, HALO, d),
                         lambda b, i, j: (b, jnp.minimum((i + 1) * halo_per_tile,
                                                         s // HALO - 1), 0, 0)),
            pl.BlockSpec((1, 3, d), lambda b, i, j: (b, 0, 0)),
            pl.BlockSpec((1, d), lambda b, i, j: (0, 0)),
            w_in_spec(0), w_in_spec(1), w_in_spec(2), w_in_spec(3),
            pl.BlockSpec((3, COL_TILE), lambda b, i, j: (0, j)),
            pl.BlockSpec((COL_TILE, d), lambda b, i, j: (j, 0)),
        ],
        out_specs=pl.BlockSpec((1, ROW_TILE, d), lambda b, i, j: (b, i, 0)),
        out_shape=jax.ShapeDtypeStruct((bsz, s, d), _F32),
        scratch_shapes=[pltpu.VMEM((ROW_TILE + 2 * HALO, d), _BF16)],
        compiler_params=pltpu.CompilerParams(
            dimension_semantics=("parallel", "parallel", "arbitrary"),
            vmem_limit_bytes=VMEM_LIMIT_BYTES),
        name="layer_a",
    )(x, x_halo, x_halo, mod, norm_g.reshape(1, d), w_in, w_in, w_in, w_in, w_conv_t, w_out)


def _layer_b_kernel(x_ref, mod_ref, ng_ref, wz_ref, wu_ref, wv_ref, lng_ref, lnb_ref,
                    ws_ref, bs_ref, wout_ref, fg_ref, o_ref,
                    h_ref, v_ref, mu_ref, rstd_ref, *, final_norm):
    j = pl.program_id(1)
    tm = x_ref.shape[0]
    n_heads = v_ref.shape[0]
    e_width = n_heads * v_ref.shape[2]

    @pl.when(j == 0)
    def _():
        h_ref[...] = _norm_modulate(x_ref[...], ng_ref[...], mod_ref).astype(_BF16)
        o_ref[...] = jnp.zeros_like(o_ref)
        mu_ref[...] = jnp.zeros_like(mu_ref)
        rstd_ref[...] = jnp.zeros_like(rstd_ref)

    @pl.when(j < n_heads)
    def _():
        v = _gelu_exact(jnp.dot(h_ref[...], wv_ref[...], preferred_element_type=_F32))
        v_ref[j] = v
        mu_ref[...] += jnp.sum(v, axis=-1, keepdims=True)
        rstd_ref[...] += jnp.sum(v * v, axis=-1, keepdims=True)

    @pl.when(j == n_heads)
    def _():
        mu = mu_ref[...] * (1.0 / e_width)
        var = rstd_ref[...] * (1.0 / e_width) - mu * mu
        mu_ref[...] = mu
        rstd_ref[...] = lax.rsqrt(var + EPS)

    @pl.when(j >= n_heads)
    def _():
        g = j - n_heads
        h = h_ref[...]
        z = jnp.dot(h, wz_ref[...], preferred_element_type=_F32)
        u = _gelu_exact(jnp.dot(h, wu_ref[...], preferred_element_type=_F32))
        vn = ((v_ref[g] - mu_ref[...]) * rstd_ref[...] * lng_ref[...] + lnb_ref[...])
        vn = vn.astype(_BF16)
        ws = ws_ref[0]
        bias = bs_ref[0]
        sv = jnp.concatenate(
            [jnp.dot(ws, vn[c * CHUNK:(c + 1) * CHUNK], preferred_element_type=_F32) + bias
             for c in range(tm // CHUNK)], axis=0)
        y = u * sv * _silu(z)
        o_ref[...] += jnp.dot(y.astype(_BF16), wout_ref[...], preferred_element_type=_F32)

    @pl.when(j == pl.num_programs(1) - 1)
    def _():
        out = x_ref[...] + mod_ref[0, 2:3, :] * o_ref[...]
        if final_norm:
            out = _rmsnorm(out, fg_ref[...])
        o_ref[...] = out


def _layer_b(x, mod, norm_g, w_in, ln_g, ln_b, w_s, b_s, w_out, final_g, *, final_norm):
    bsz, s, d = x.shape
    e = w_out.shape[0]
    heads = w_s.shape[0]
    head_dim = e // heads
    tiles_per_seq = s // ROW_TILE
    x2 = x.reshape(bsz * s, d)

    def head(j):
        return jnp.maximum(j - heads, 0)

    out = pl.pallas_call(
        functools.partial(_layer_b_kernel, final_norm=final_norm),
        grid=(bsz * tiles_per_seq, 2 * heads),
        in_specs=[
            pl.BlockSpec((ROW_TILE, d), lambda i, j: (i, 0)),
            pl.BlockSpec((1, 3, d), lambda i, j: (i // tiles_per_seq, 0, 0)),
            pl.BlockSpec((1, d), lambda i, j: (0, 0)),
            pl.BlockSpec((d, head_dim), lambda i, j: (0, head(j))),
            pl.BlockSpec((d, head_dim), lambda i, j: (0, heads + head(j))),
            pl.BlockSpec((d, head_dim), lambda i, j: (0, 2 * heads + jnp.minimum(j, heads - 1))),
            pl.BlockSpec((1, head_dim), lambda i, j: (0, head(j))),
            pl.BlockSpec((1, head_dim), lambda i, j: (0, head(j))),
            pl.BlockSpec((1, CHUNK, CHUNK), lambda i, j: (head(j), 0, 0)),
            pl.BlockSpec((1, CHUNK, 1), lambda i, j: (head(j), 0, 0)),
            pl.BlockSpec((head_dim, d), lambda i, j: (head(j), 0)),
            pl.BlockSpec((1, d), lambda i, j: (0, 0)),
        ],
        out_specs=pl.BlockSpec((ROW_TILE, d), lambda i, j: (i, 0)),
        out_shape=jax.ShapeDtypeStruct((bsz * s, d), _F32),
        scratch_shapes=[
            pltpu.VMEM((ROW_TILE, d), _BF16),
            pltpu.VMEM((heads, ROW_TILE, head_dim), _F32),
            pltpu.VMEM((ROW_TILE, 1), _F32),
            pltpu.VMEM((ROW_TILE, 1), _F32),
        ],
        compiler_params=pltpu.CompilerParams(
            dimension_semantics=("parallel", "arbitrary"),
            vmem_limit_bytes=VMEM_LIMIT_BYTES),
        name="layer_b",
    )(x2, mod, norm_g.reshape(1, d), w_in, w_in, w_in, ln_g.reshape(1, e), ln_b.reshape(1, e),
      w_s, b_s.reshape(heads, CHUNK, 1), w_out, final_g.reshape(1, d))
    return out.reshape(bsz, s, d)


def kernel(x, c, norm_g, w_mod, b_mod, a_w_in, a_w_conv, a_w_out, b_w_in, b_ln_g, b_ln_b,
           b_w_s, b_b_s, b_w_out, final_g):
    depth, d = norm_g.shape
    bsz = x.shape[0]
    assert b_w_s.shape[1:] == (SG_HEADS, CHUNK, CHUNK)
    mod = _modulation(c, w_mod, b_mod).reshape(depth, bsz, 3, d)
    for i in range(depth):
        k = i // 2
        if i % 2 == 0:
            x = _layer_a(x, mod[i], norm_g[i], a_w_in[k].astype(_BF16),
                         a_w_conv[k].T, a_w_out[k].astype(_BF16))
        else:
            x = _layer_b(x, mod[i], norm_g[i], b_w_in[k].astype(_BF16), b_ln_g[k], b_ln_b[k],
                         b_w_s[k].astype(_BF16), b_b_s[k], b_w_out[k].astype(_BF16), final_g,
                         final_norm=(i == depth - 1))
    return x
```
